```python
import math
import jax, jax.numpy as jnp
from jax import lax
import numpy as np

D_MODEL = 1024
BATCH = 16
SEQ = 2048
DEPTH = 2
DEC_BATCH = 32
DEC_SEQ = 1
PAST_LEN = 16384
PAGE_SIZE = 128

MIX_WIDTH = D_MODEL
SB_HEAD_DIM = 64
SB_WIDTH = MIX_WIDTH // 2
SB_HEADS = SB_WIDTH // SB_HEAD_DIM
SB_BLOCK = 128
SB_BIAS_INIT = -7.0
GLA_HEADS = 4
GLA_V_WIDTH = MIX_WIDTH // 4
GLA_DV = GLA_V_WIDTH // GLA_HEADS
GLA_DK = GLA_DV // 2
GLA_K_WIDTH = GLA_HEADS * GLA_DK
GLA_RANK = 16
GLA_TAU = 16.0
GLA_CHUNK = 64
LRU_WIDTH = MIX_WIDTH // 4
LRU_BLOCKS = 4
LRU_BLOCK_W = LRU_WIDTH // LRU_BLOCKS
LRU_C = 8.0
CONV_W = 4
D_FF = 4 * D_MODEL
RMS_EPS = 1e-6
IN_SIZES = (SB_WIDTH, SB_WIDTH, SB_WIDTH, GLA_K_WIDTH, GLA_K_WIDTH, GLA_V_WIDTH, GLA_V_WIDTH, GLA_RANK, LRU_WIDTH, LRU_WIDTH)
IN_WIDTH = sum(IN_SIZES)
IN_SPLITS = tuple(int(s) for s in np.cumsum(IN_SIZES)[:-1])

kernel_name = "hymba_sb_gla_rglru_step"


def _rmsnorm(x, g):
    x32 = x.astype(jnp.float32)
    y = x32 * lax.rsqrt(jnp.mean(x32 * x32, axis=-1, keepdims=True) + RMS_EPS)
    return (y * g.astype(jnp.float32)).astype(x.dtype)


def _stick_breaking(q, k, v, q_offset, bias):
    B, Lq, H, d = q.shape
    Lk = k.shape[1]
    qb_len = min(SB_BLOCK, Lq)
    nb = -(-Lq // qb_len)
    pad = nb * qb_len - Lq
    q = jnp.pad(q, ((0, 0), (0, pad), (0, 0), (0, 0)))
    q_blocks = q.reshape(B, nb, qb_len, H, d).transpose(1, 0, 2, 3, 4)
    key_pos = jnp.arange(Lk)
    scale = SB_HEAD_DIM ** -0.5
    b32 = bias.astype(jnp.float32)[None, :, None, None]

    def one_block(args):
        qi, bi = args
        q_pos = q_offset + bi * qb_len + jnp.arange(qb_len)
        mask = key_pos[None, :] < q_pos[:, None]
        z = jnp.einsum('bqhd,bkhd->bhqk', qi, k).astype(jnp.float32) * scale + b32
        neg_log_keep = jnp.where(mask, jax.nn.softplus(z), 0.0)
        between = lax.cumsum(neg_log_keep, axis=3, reverse=True) - neg_log_keep
        w = jnp.where(mask, jnp.exp(jax.nn.log_sigmoid(z) - between), 0.0)
        return jnp.einsum('bhqk,bkhd->bqhd', w.astype(v.dtype), v)

    out = lax.map(one_block, (q_blocks, jnp.arange(nb)))
    return out.transpose(1, 0, 2, 3, 4).reshape(B, nb * qb_len, H, d)[:, :Lq]


def _gla(q, k, v, log_a, s0):
    dt = v.dtype
    B, L, H, dk = q.shape
    dv = v.shape[-1]
    C = min(GLA_CHUNK, L)
    n = -(-L // C)
    pad = n * C - L

    def chunks(t):
        t = jnp.pad(t.astype(jnp.float32), ((0, 0), (0, pad), (0, 0), (0, 0)))
        return t.reshape(B, n, C, H, t.shape[-1]).transpose(1, 0, 2, 3, 4)

    tri = jnp.tril(jnp.ones((C, C), dtype=bool))[None, :, :, None, None]

    def step(S, inp):
        qc, kc, vc, gc = inp
        b = jnp.cumsum(gc, axis=1)
        o = jnp.einsum('bthk,bhkv->bthv', qc * jnp.exp(b), S)
        decay = jnp.exp(jnp.where(tri, b[:, :, None] - b[:, None], -jnp.inf))
        att = jnp.einsum('bthk,bshk,btshk->bhts', qc, kc, decay)
        o = o + jnp.einsum('bhts,bshv->bthv', att, vc)
        b_last = b[:, -1]
        S = S * jnp.exp(b_last)[..., None] + jnp.einsum('bshk,bshv->bhkv', kc * jnp.exp(b_last[:, None] - b), vc)
        return S, o

    S, o = lax.scan(step, s0.astype(jnp.float32), (chunks(q), chunks(k), chunks(v), chunks(log_a)))
    o = o.transpose(1, 0, 2, 3, 4).reshape(B, n * C, H, dv)[:, :L]
    return o.astype(dt), S.astype(dt)


def _causal_conv(x, buf, w, b):
    L = x.shape[1]
    xp = jnp.concatenate([buf.astype(x.dtype), x], axis=1)
    y = sum(xp[:, j:j + L] * w[j] for j in range(CONV_W)) + b
    return y, xp[:, -(CONV_W - 1):]


def _lin_combine(e1, e2):
    a1, b1 = e1
    a2, b2 = e2
    return a1 * a2, a2 * b1 + b2


def _rg_lru(x, h0, w_a, b_a, w_x, b_x, lam):
    B, L, C = x.shape
    xb = x.reshape(B, L, LRU_BLOCKS, LRU_BLOCK_W)
    r = jax.nn.sigmoid(jnp.einsum('blnc,ncd->blnd', xb, w_a).reshape(B, L, C) + b_a).astype(jnp.float32)
    i = jax.nn.sigmoid(jnp.einsum('blnc,ncd->blnd', xb, w_x).reshape(B, L, C) + b_x).astype(jnp.float32)
    log_a = LRU_C * r * jax.nn.log_sigmoid(lam.astype(jnp.float32))
    a = jnp.exp(log_a)
    u = jnp.sqrt(-jnp.expm1(2.0 * log_a)) * (i * x.astype(jnp.float32))
    u = u.at[:, 0].add(a[:, 0] * h0.astype(jnp.float32))
    _, hs = lax.associative_scan(_lin_combine, (a, u), axis=1)
    return hs.astype(x.dtype), hs[:, -1].astype(x.dtype)


def _token_mixers(h, past_k, past_v, gla_s0, conv_buf, lru_h0, w_in, sb_bias, sb_norm, gla_wg2, gla_bg, gla_norm,
                  conv_w, conv_b, lru_wa, lru_ba, lru_wx, lru_bx, lru_lambda, lru_norm, w_out):
    B, L, _ = h.shape
    proj = h @ w_in
    sq, sk, sv, gq, gk, gv, gg, glr, lx, lg = jnp.split(proj, IN_SPLITS, axis=-1)
    sq = sq.reshape(B, L, SB_HEADS, SB_HEAD_DIM)
    sk = sk.reshape(B, L, SB_HEADS, SB_HEAD_DIM)
    sv = sv.reshape(B, L, SB_HEADS, SB_HEAD_DIM)
    k_all = jnp.concatenate([past_k.astype(sk.dtype), sk], axis=1)
    v_all = jnp.concatenate([past_v.astype(sv.dtype), sv], axis=1)
    o_sb = _stick_breaking(sq, k_all, v_all, past_k.shape[1], sb_bias).reshape(B, L, SB_WIDTH)
    o_sb = _rmsnorm(o_sb, sb_norm)
    gq = gq.reshape(B, L, GLA_HEADS, GLA_DK) * (GLA_DK ** -0.5)
    gk = gk.reshape(B, L, GLA_HEADS, GLA_DK)
    gv = gv.reshape(B, L, GLA_HEADS, GLA_DV)
    gate = (glr @ gla_wg2 + gla_bg).reshape(B, L, GLA_HEADS, GLA_DK)
    log_alpha = jax.nn.log_sigmoid(gate.astype(jnp.float32)) / GLA_TAU
    o_gla, s_new = _gla(gq, gk, gv, log_alpha, gla_s0)
    o_gla = _rmsnorm(o_gla, gla_norm).reshape(B, L, GLA_V_WIDTH) * jax.nn.silu(gg)
    xc, buf_new = _causal_conv(lx, conv_buf, conv_w, conv_b)
    hh, h_last = _rg_lru(xc, lru_h0, lru_wa, lru_ba, lru_wx, lru_bx, lru_lambda)
    o_lru = _rmsnorm(hh * jax.nn.gelu(lg), lru_norm)
    out = jnp.concatenate([o_sb, o_gla, o_lru], axis=-1) @ w_out
    return out, sk, sv, s_new, buf_new, h_last


def _trunk(x, cache_k, cache_v, page_table, gla_s0, conv0, lru0, ln1, w_in, sb_bias, sb_norm, gla_wg2, gla_bg,
           gla_norm, conv_w, conv_b, lru_wa, lru_ba, lru_wx, lru_bx, lru_lambda, lru_norm, w_out, ln2, w_up, w_down,
           ln_f):
    B = x.shape[0]
    ks, vs, ss, cs, hs = [], [], [], [], []
    for l in range(DEPTH):
        if cache_k is None:
            past_k = jnp.zeros((B, 0, SB_HEADS, SB_HEAD_DIM), x.dtype)
            past_v = jnp.zeros((B, 0, SB_HEADS, SB_HEAD_DIM), x.dtype)
        else:
            past_k = cache_k[l][page_table].reshape(B, -1, SB_HEADS, SB_HEAD_DIM)
            past_v = cache_v[l][page_table].reshape(B, -1, SB_HEADS, SB_HEAD_DIM)
        h = _rmsnorm(x, ln1[l])
        mix, k_l, v_l, s_l, c_l, h_l = _token_mixers(
            h, past_k, past_v, gla_s0[l], conv0[l], lru0[l], w_in[l], sb_bias[l], sb_norm[l], gla_wg2[l],
            gla_bg[l], gla_norm[l], conv_w[l], conv_b[l], lru_wa[l], lru_ba[l], lru_wx[l], lru_bx[l],
            lru_lambda[l], lru_norm[l], w_out[l])
        x = x + mix
        hm = _rmsnorm(x, ln2[l])
        x = x + jnp.square(jax.nn.relu(hm @ w_up[l])) @ w_down[l]
        ks.append(k_l); vs.append(v_l); ss.append(s_l); cs.append(c_l); hs.append(h_l)
    return (_rmsnorm(x, ln_f), jnp.stack(ks), jnp.stack(vs), jnp.stack(ss), jnp.stack(cs), jnp.stack(hs))


def setup_inputs(seed: int = 0) -> dict:
    key = jax.random.key(seed)
    ks = jax.random.split(key, 32)
    f = jnp.float32
    n_pages = PAST_LEN // PAGE_SIZE
    n_pool = (DEC_BATCH * n_pages * 5) // 4

    def nrm(k, shape, s):
        return jax.random.normal(k, shape, f) * s

    page_table = jax.random.permutation(ks[4], n_pool)[: DEC_BATCH * n_pages].reshape(DEC_BATCH, n_pages).astype(jnp.int32)
    u = jax.random.uniform(ks[20], (DEPTH, LRU_WIDTH), f, 0.9, 0.999)
    s = u ** (1.0 / LRU_C)
    lru_lambda = jnp.log(s) - jnp.log1p(-s)
    return {
        "x_prompt": nrm(ks[0], (BATCH, SEQ, D_MODEL), 1.0),
        "x_sample": nrm(ks[1], (DEC_BATCH, DEC_SEQ, D_MODEL), 1.0),
        "cache_sb_k": nrm(ks[2], (DEPTH, n_pool, PAGE_SIZE, SB_HEADS, SB_HEAD_DIM), 1.0),
        "cache_sb_v": nrm(ks[3], (DEPTH, n_pool, PAGE_SIZE, SB_HEADS, SB_HEAD_DIM), 1.0),
        "page_table": page_table,
        "state_gla": nrm(ks[5], (DEPTH, DEC_BATCH, GLA_HEADS, GLA_DK, GLA_DV), 0.5),
        "state_conv": nrm(ks[6], (DEPTH, DEC_BATCH, CONV_W - 1, LRU_WIDTH), 1.0),
        "state_lru": nrm(ks[7], (DEPTH, DEC_BATCH, LRU_WIDTH), 0.5),
        "ln1": 1.0 + nrm(ks[8], (DEPTH, D_MODEL), 0.05),
        "w_in": nrm(ks[9], (DEPTH, D_MODEL, IN_WIDTH), D_MODEL ** -0.5),
        "sb_bias": SB_BIAS_INIT + nrm(ks[27], (DEPTH, SB_HEADS), 0.3),
        "sb_norm": 1.0 + nrm(ks[10], (DEPTH, SB_WIDTH), 0.05),
        "gla_wg2": nrm(ks[11], (DEPTH, GLA_RANK, GLA_K_WIDTH), GLA_RANK ** -0.5),
        "gla_bg": nrm(ks[12], (DEPTH, GLA_K_WIDTH), 0.01),
        "gla_norm": 1.0 + nrm(ks[13], (DEPTH, GLA_DV), 0.05),
        "conv_w": nrm(ks[14], (DEPTH, CONV_W, LRU_WIDTH), CONV_W ** -0.5),
        "conv_b": nrm(ks[15], (DEPTH, LRU_WIDTH), 0.01),
        "lru_wa": nrm(ks[16], (DEPTH, LRU_BLOCKS, LRU_BLOCK_W, LRU_BLOCK_W), LRU_BLOCK_W ** -0.5),
        "lru_ba": nrm(ks[17], (DEPTH, LRU_WIDTH), 0.01),
        "lru_wx": nrm(ks[18], (DEPTH, LRU_BLOCKS, LRU_BLOCK_W, LRU_BLOCK_W), LRU_BLOCK_W ** -0.5),
        "lru_bx": nrm(ks[19], (DEPTH, LRU_WIDTH), 0.01),
        "lru_lambda": lru_lambda,
        "lru_norm": 1.0 + nrm(ks[21], (DEPTH, LRU_WIDTH), 0.05),
        "w_out": nrm(ks[22], (DEPTH, MIX_WIDTH, D_MODEL), MIX_WIDTH ** -0.5),
        "ln2": 1.0 + nrm(ks[23], (DEPTH, D_MODEL), 0.05),
        "w_up": nrm(ks[24], (DEPTH, D_MODEL, D_FF), D_MODEL ** -0.5),
        "w_down": nrm(ks[25], (DEPTH, D_FF, D_MODEL), D_FF ** -0.5),
        "ln_f": 1.0 + nrm(ks[26], (D_MODEL,), 0.05),
    }


def reference(x_prompt, x_sample, cache_sb_k, cache_sb_v, page_table, state_gla, state_conv, state_lru,
              ln1, w_in, sb_bias, sb_norm, gla_wg2, gla_bg, gla_norm, conv_w, conv_b, lru_wa, lru_ba, lru_wx,
              lru_bx, lru_lambda, lru_norm, w_out, ln2, w_up, w_down, ln_f):
    weights = (ln1, w_in, sb_bias, sb_norm, gla_wg2, gla_bg, gla_norm, conv_w, conv_b, lru_wa, lru_ba, lru_wx,
               lru_bx, lru_lambda, lru_norm, w_out, ln2, w_up, w_down, ln_f)
    bp = x_prompt.shape[0]
    dt = x_prompt.dtype
    gla0 = jnp.zeros((DEPTH, bp, GLA_HEADS, GLA_DK, GLA_DV), dt)
    conv0 = jnp.zeros((DEPTH, bp, CONV_W - 1, LRU_WIDTH), dt)
    lru0 = jnp.zeros((DEPTH, bp, LRU_WIDTH), dt)
    y_prompt, k_p, v_p, gla_p, conv_p, lru_p = _trunk(x_prompt, None, None, None, gla0, conv0, lru0, *weights)
    y_sample, k_s, v_s, gla_s, conv_s, lru_s = _trunk(x_sample, cache_sb_k, cache_sb_v, page_table, state_gla,
                                                      state_conv, state_lru, *weights)
    return (y_prompt, y_sample, k_p, v_p, k_s, v_s, gla_p, gla_s, conv_p, conv_s, lru_p, lru_s)
```

```python
import functools
import math

import numpy as np
import jax
import jax.numpy as jnp
from jax import lax
from jax.experimental import pallas as pl
from jax.experimental.pallas import tpu as pltpu

F32 = jnp.float32
BF16 = jnp.bfloat16

LANES_V7X = 128
VMEM_BYTES_V7X = 64 * 1024 * 1024

D_MODEL = 1024
SB_HEAD_DIM = 64
SB_WIDTH = D_MODEL // 2
SB_HEADS = SB_WIDTH // SB_HEAD_DIM
GLA_HEADS = 4
GLA_V_WIDTH = D_MODEL // 4
GLA_DV = GLA_V_WIDTH // GLA_HEADS
GLA_DK = GLA_DV // 2
GLA_K_WIDTH = GLA_HEADS * GLA_DK
GLA_RANK = 16
GLA_TAU = 16.0
GLA_CHUNK = 64
LRU_WIDTH = D_MODEL // 4
LRU_BLOCKS = 4
LRU_C = 8.0
CONV_W = 4
D_FF = 4 * D_MODEL
RMS_EPS = 1e-6

GLR_PAD = LANES_V7X
SB_COLS = 3 * SB_WIDTH
GLA_COLS = 2 * GLA_K_WIDTH + 2 * GLA_V_WIDTH + GLR_PAD
LRU_COLS = 2 * LRU_WIDTH
IN_COLS = SB_COLS + GLA_COLS + LRU_COLS


def _vmem_limit(nbytes):
    return int(min(max(2 * nbytes, 32 * 1024 * 1024), VMEM_BYTES_V7X - 8 * 1024 * 1024))


def _rms(x, g):
    ms = jnp.mean(x * x, axis=-1, keepdims=True)
    return x * lax.rsqrt(ms + RMS_EPS) * g


def _split_bf16(x):
    hi = x.astype(BF16)
    lo = (x - hi.astype(F32)).astype(BF16)
    return hi, lo


def _log_sigmoid(x):
    return jnp.minimum(x, 0.0) - jnp.log(1.0 + jnp.exp(-jnp.abs(x)))


def _sigmoid(x):
    return 1.0 / (1.0 + jnp.exp(-x))


SB_TILE = 256


def _proj_common(h, w_ref, qb_ref, gla_ref, lru_ref):
    def proj(lo, hi):
        return jnp.dot(h, w_ref[:, lo:hi], preferred_element_type=F32)

    qb_ref[...] = (proj(0, SB_WIDTH) * (SB_HEAD_DIM ** -0.5)).astype(BF16).reshape(qb_ref.shape)
    o = SB_COLS
    gq = proj(o, o + GLA_K_WIDTH) * (GLA_DK ** -0.5)
    rest = proj(o + GLA_K_WIDTH, o + GLA_COLS)
    gla_ref[...] = jnp.concatenate([gq, rest], axis=1).reshape(gla_ref.shape)
    o = SB_COLS + GLA_COLS
    lru_ref[...] = proj(o, o + LRU_COLS).reshape(lru_ref.shape)
    return proj


def _in_proj_prompt_kernel(x_ref, g_ref, w_ref, wkvt_ref, qb_ref, kt_ref, vt_ref, ktb_ref, vtb_ref,
                           gla_ref, lru_ref, *, key_tile):
    h = _rms(x_ref[0], g_ref[...]).astype(BF16)
    _proj_common(h, w_ref, qb_ref, gla_ref, lru_ref)
    for t_ref, tb_ref, r0 in ((kt_ref, ktb_ref, 0), (vt_ref, vtb_ref, SB_WIDTH)):
        xt = lax.dot_general(wkvt_ref[r0:r0 + SB_WIDTH, :], h, (((1,), (1,)), ((), ())),
                             preferred_element_type=F32)
        t_ref[0] = xt
        for c in range(xt.shape[1] // key_tile):
            tb_ref[0, c] = xt[:, c * key_tile:(c + 1) * key_tile].astype(BF16)


def _in_proj_prompt(x, ln, w_bf, wkvt_bf, key_tile):
    batch, seq, _ = x.shape
    tm = min(512, seq)
    assert seq % tm == 0 and tm % key_tile == 0
    nt = seq // tm
    const = lambda b, i: (0, 0)
    tok = lambda b, i: (b, i, 0)
    out_shapes = (
        jax.ShapeDtypeStruct((batch, seq, SB_WIDTH), BF16),
        jax.ShapeDtypeStruct((batch, SB_WIDTH, seq), F32),
        jax.ShapeDtypeStruct((batch, SB_WIDTH, seq), F32),
        jax.ShapeDtypeStruct((batch, seq // key_tile, SB_WIDTH, key_tile), BF16),
        jax.ShapeDtypeStruct((batch, seq // key_tile, SB_WIDTH, key_tile), BF16),
        jax.ShapeDtypeStruct((batch, seq, GLA_COLS), F32),
        jax.ShapeDtypeStruct((batch, seq, LRU_COLS), F32),
    )
    tspec = pl.BlockSpec((1, SB_WIDTH, tm), lambda b, i: (b, 0, i))
    tbspec = pl.BlockSpec((1, tm // key_tile, SB_WIDTH, key_tile), lambda b, i: (b, i, 0, 0))
    out_specs = (pl.BlockSpec((1, tm, SB_WIDTH), tok), tspec, tspec, tbspec, tbspec,
                 pl.BlockSpec((1, tm, GLA_COLS), tok), pl.BlockSpec((1, tm, LRU_COLS), tok))
    est = 2 * (tm * D_MODEL * 4 + D_MODEL * (IN_COLS + 2 * SB_WIDTH) * 2
               + tm * (SB_WIDTH * (2 + 8 + 4) + (GLA_COLS + LRU_COLS) * 4))
    return pl.pallas_call(
        functools.partial(_in_proj_prompt_kernel, key_tile=key_tile),
        grid=(batch, nt),
        in_specs=[pl.BlockSpec((1, tm, D_MODEL), tok),
                  pl.BlockSpec((1, D_MODEL), const),
                  pl.BlockSpec((D_MODEL, IN_COLS), const),
                  pl.BlockSpec((2 * SB_WIDTH, D_MODEL), const)],
        out_specs=out_specs,
        out_shape=out_shapes,
        compiler_params=pltpu.CompilerParams(
            dimension_semantics=("parallel", "parallel"), vmem_limit_bytes=_vmem_limit(est)),
        name="in_proj_prompt",
    )(x, ln.reshape(1, D_MODEL), w_bf, wkvt_bf)


def _in_proj_step_kernel(x_ref, g_ref, w_ref, qb_ref, k_ref, v_ref, gla_ref, lru_ref):
    h = _rms(x_ref[...], g_ref[...]).astype(BF16)
    proj = _proj_common(h, w_ref, qb_ref, gla_ref, lru_ref)
    k_ref[...] = proj(SB_WIDTH, 2 * SB_WIDTH)
    v_ref[...] = proj(2 * SB_WIDTH, 3 * SB_WIDTH)


def _in_proj_step(x2d, ln, w_bf):
    t = x2d.shape[0]
    out_shapes = (
        jax.ShapeDtypeStruct((t, SB_WIDTH), BF16),
        jax.ShapeDtypeStruct((t, SB_WIDTH), F32),
        jax.ShapeDtypeStruct((t, SB_WIDTH), F32),
        jax.ShapeDtypeStruct((t, GLA_COLS), F32),
        jax.ShapeDtypeStruct((t, LRU_COLS), F32),
    )
    return pl.pallas_call(
        _in_proj_step_kernel,
        out_shape=out_shapes,
        compiler_params=pltpu.CompilerParams(
            vmem_limit_bytes=_vmem_limit(D_MODEL * IN_COLS * 2)),
        name="in_proj_step",
    )(x2d, ln.reshape(1, D_MODEL), w_bf)


def _softplus(z):
    return jnp.maximum(z, 0.0) + jnp.log(1.0 + jnp.exp(-jnp.abs(z)))


def _sb_tile(qh, bias, kt, vt, upper, later, mask):
    z = jnp.dot(qh, kt, preferred_element_type=F32) + bias
    sp = _softplus(z)
    if mask is not None:
        sp = jnp.where(mask, sp, 0.0)
    hi, lo = _split_bf16(sp)
    excl = (jnp.dot(hi, upper, preferred_element_type=F32)
            + jnp.dot(lo, upper, preferred_element_type=F32))
    w = jnp.exp(z - sp - excl - later)
    if mask is not None:
        w = jnp.where(mask, w, 0.0)
    pv = lax.dot_general(w.astype(BF16), vt, (((1,), (1,)), ((), ())), preferred_element_type=F32)
    later = later + excl[:, 0:1] + sp[:, 0:1]
    return pv, later


def _sb_attn_kernel(bias_ref, q_ref, k_ref, v_ref, o_ref, *, tile):
    hp = pl.program_id(1)
    qi = pl.program_id(2)
    q = q_ref[0].astype(F32)
    lane = lax.broadcasted_iota(jnp.int32, (tile, LANES_V7X), 1)
    first_head = lane < SB_HEAD_DIM
    qs = (jnp.where(first_head, q, 0.0).astype(BF16), jnp.where(first_head, 0.0, q).astype(BF16))
    biases = (bias_ref[2 * hp], bias_ref[2 * hp + 1])
    r_i = lax.broadcasted_iota(jnp.int32, (tile, tile), 0)
    c_i = lax.broadcasted_iota(jnp.int32, (tile, tile), 1)
    upper = jnp.where(r_i > c_i, 1.0, 0.0).astype(BF16)
    causal = c_i < r_i

    kd = k_ref[0, qi]
    vd = v_ref[0, qi]
    carry = []
    for h in range(2):
        pv, later = _sb_tile(qs[h], biases[h], kd, vd, upper, jnp.zeros((tile, 1), F32), causal)
        carry += [pv, later]

    def body(it, carry):
        j = qi - 1 - it
        kb = k_ref[0, j]
        vb = v_ref[0, j]
        out = []
        for h in range(2):
            pv, later = _sb_tile(qs[h], biases[h], kb, vb, upper, carry[2 * h + 1], None)
            out += [carry[2 * h] + pv, later]
        return tuple(out)

    carry = lax.fori_loop(0, qi, body, tuple(carry))
    o_ref[0] = jnp.where(first_head, carry[0], carry[2])


def _sb_attention_prompt(qb, ktb, vtb, sb_bias):
    batch, seq, _ = qb.shape
    nq, tile = ktb.shape[1], ktb.shape[3]
    n_pairs = SB_WIDTH // LANES_V7X
    est = 2 * (2 * seq * LANES_V7X * 2 + tile * LANES_V7X * 6) + 24 * tile * tile * 4
    kv_spec = pl.BlockSpec((1, nq, LANES_V7X, tile), lambda b, hp, qi: (b, 0, hp, 0))
    return pl.pallas_call(
        functools.partial(_sb_attn_kernel, tile=tile),
        grid=(batch, n_pairs, nq),
        in_specs=[pl.BlockSpec(memory_space=pltpu.SMEM),
                  pl.BlockSpec((1, tile, LANES_V7X), lambda b, hp, qi: (b, qi, hp)),
                  kv_spec, kv_spec],
        out_specs=pl.BlockSpec((1, tile, LANES_V7X), lambda b, hp, qi: (b, qi, hp)),
        out_shape=jax.ShapeDtypeStruct((batch, seq, SB_WIDTH), F32),
        compiler_params=pltpu.CompilerParams(
            dimension_semantics=("parallel", "parallel", "parallel"),
            vmem_limit_bytes=_vmem_limit(est)),
        name="sb_attn_prompt",
    )(sb_bias, qb, ktb, vtb)


def _sb_decode_kernel(pt_ref, q_ref, bias_ref, *refs, pages_per_step, page):
    k_refs = refs[:pages_per_step]
    v_refs = refs[pages_per_step:2 * pages_per_step]
    o_ref = refs[2 * pages_per_step]
    acc_ref, later_ref = refs[2 * pages_per_step + 1:]
    step = pl.program_id(1)

    @pl.when(step == 0)
    def _():
        acc_ref[...] = jnp.zeros_like(acc_ref)
        later_ref[...] = jnp.zeros_like(later_ref)

    head = lax.broadcasted_iota(jnp.int32, (SB_HEADS, SB_WIDTH), 0)
    lane = lax.broadcasted_iota(jnp.int32, (SB_HEADS, SB_WIDTH), 1)
    own = (lane >= head * SB_HEAD_DIM) & (lane < (head + 1) * SB_HEAD_DIM)
    q = jnp.broadcast_to(q_ref[0].astype(F32), (SB_HEADS, SB_WIDTH))
    qh = jnp.where(own, q, 0.0).astype(BF16)
    bias = bias_ref[...]
    r_i = lax.broadcasted_iota(jnp.int32, (page, page), 0)
    c_i = lax.broadcasted_iota(jnp.int32, (page, page), 1)
    upper = jnp.where(r_i > c_i, 1.0, 0.0).astype(BF16)

    acc = acc_ref[...]
    later = later_ref[...]
    for i in range(pages_per_step):
        kp = k_refs[i][0].astype(BF16)
        vp = v_refs[i][0].astype(BF16)
        pv, later = _sb_tile(qh, bias, kp, vp, upper, later, None)
        acc = acc + pv
    acc_ref[...] = acc
    later_ref[...] = later

    @pl.when(step == pl.num_programs(1) - 1)
    def _():
        o_ref[0] = jnp.sum(jnp.where(own, acc, 0.0), axis=0, keepdims=True)


def _sb_attention_decode(qb, cache_k, cache_v, page_table, sb_bias, layer):
    batch, n_pages = page_table.shape
    depth, n_pool, page = cache_k.shape[:3]
    pages_per_step = math.gcd(n_pages, 8)
    n_steps = n_pages // pages_per_step
    ck = jnp.transpose(cache_k, (0, 1, 3, 4, 2)).reshape(depth * n_pool, SB_WIDTH, page)
    cv = jnp.transpose(cache_v, (0, 1, 3, 4, 2)).reshape(depth * n_pool, SB_WIDTH, page)
    base = layer * n_pool

    def page_spec(i):
        def index_map(b, s, pt):
            logical = n_pages - 1 - (s * pages_per_step + i)
            return (base + pt[b * n_pages + logical], 0, 0)
        return pl.BlockSpec((1, SB_WIDTH, page), index_map)

    grid_spec = pltpu.PrefetchScalarGridSpec(
        num_scalar_prefetch=1,
        grid=(batch, n_steps),
        in_specs=[pl.BlockSpec((1, 1, SB_WIDTH), lambda b, s, pt: (b, 0, 0)),
                  pl.BlockSpec((SB_HEADS, page), lambda b, s, pt: (0, 0))]
                 + [page_spec(i) for i in range(pages_per_step)] * 2,
        out_specs=pl.BlockSpec((1, 1, SB_WIDTH), lambda b, s, pt: (b, 0, 0)),
        scratch_shapes=[pltpu.VMEM((SB_HEADS, SB_WIDTH), F32), pltpu.VMEM((SB_HEADS, 1), F32)],
    )
    est = 2 * 2 * pages_per_step * page * SB_WIDTH * 4
    out = pl.pallas_call(
        functools.partial(_sb_decode_kernel, pages_per_step=pages_per_step, page=page),
        grid_spec=grid_spec,
        out_shape=jax.ShapeDtypeStruct((batch, 1, SB_WIDTH), F32),
        compiler_params=pltpu.CompilerParams(
            dimension_semantics=("parallel", "arbitrary"), vmem_limit_bytes=_vmem_limit(est)),
        name="sb_attn_decode",
    )(page_table.reshape(-1), qb.reshape(batch, 1, SB_WIDTH),
      jnp.broadcast_to(sb_bias[:, None], (SB_HEADS, page)),
      *([ck] * pages_per_step), *([cv] * pages_per_step))
    return out.reshape(batch, SB_WIDTH)


GLA_LEVELS = int(math.log2(GLA_CHUNK))


def _gla_constants():
    c = GLA_CHUNK
    idx = np.arange(c)
    r, j = idx[:, None], idx[None, :]
    blocks = [(j <= r)]
    masks = []
    for level in range(GLA_LEVELS):
        m = c >> level
        half = m // 2
        mid = (r // m) * m + half
        second = r >= mid
        blocks.append(np.where(second, (j >= mid) & (j <= r), (j > r) & (j < mid)))
        same = (r // m) == (j // m)
        masks.append(same & ((r % m) >= half) & ((j % m) < half))
    masks.append(r == j)
    blocks.append(j > r)
    sums = np.concatenate(blocks, axis=0).astype(np.float32)
    pair_masks = np.stack([np.tile(m, (1, GLA_HEADS)) for m in masks]).astype(np.float32)
    hv = np.arange(GLA_V_WIDTH)[:, None] // GLA_DV
    hk = np.arange(GLA_K_WIDTH)[None, :] // GLA_DK
    state_mask = (hv == hk).astype(np.float32)
    hv2 = np.arange(GLA_V_WIDTH)
    head_ones = (hv2[:, None] // GLA_DV == hv2[None, :] // GLA_DV).astype(np.float32)
    return sums, pair_masks, state_mask, head_ones


def _gla_kernel(x_ref, s0_ref, sums_ref, pmask_ref, smask_ref, hones_ref, wg_ref, bg_ref, norm_ref,
                o_ref, sT_ref, state_ref, *, n_chunks, valid_len):
    c = GLA_CHUNK
    tb = pl.program_id(1)

    @pl.when(tb == 0)
    def _():
        state_ref[...] = s0_ref[0]

    lane_k = lax.broadcasted_iota(jnp.int32, (c, GLA_K_WIDTH), 1) // GLA_DK
    lane_v = lax.broadcasted_iota(jnp.int32, (c, GLA_V_WIDTH), 1) // GLA_DV
    row = lax.broadcasted_iota(jnp.int32, (c, GLA_K_WIDTH), 0)
    sums = sums_ref[...]
    smask = smask_ref[...]
    hones = hones_ref[...]
    wg = wg_ref[...]
    bg = bg_ref[...]
    norm = norm_ref[...]

    def stack_heads(x, lane_head, n):
        return jnp.concatenate(
            [jnp.where(lane_head == h, x, 0.0).astype(BF16) for h in range(n)], axis=0)

    def chunk(ci, _):
        r0 = pl.multiple_of(ci * c, c)
        q = x_ref[0, pl.ds(r0, c), 0:GLA_K_WIDTH]
        k = x_ref[0, pl.ds(r0, c), GLA_K_WIDTH:2 * GLA_K_WIDTH]
        v = x_ref[0, pl.ds(r0, c), 2 * GLA_K_WIDTH:2 * GLA_K_WIDTH + GLA_V_WIDTH]
        gg = x_ref[0, pl.ds(r0, c), 2 * GLA_K_WIDTH + GLA_V_WIDTH:2 * GLA_K_WIDTH + 2 * GLA_V_WIDTH]
        glr = x_ref[0, pl.ds(r0, c), 2 * GLA_K_WIDTH + 2 * GLA_V_WIDTH:GLA_COLS]
        gate = jnp.dot(glr.astype(BF16), wg, preferred_element_type=F32) + bg
        g = _log_sigmoid(gate) * (1.0 / GLA_TAU)
        if valid_len is not None:
            g = jnp.where(row + tb * (n_chunks * c) + r0 < valid_len, g, 0.0)
        g1 = g.astype(BF16)
        rem = g - g1.astype(F32)
        g2 = rem.astype(BF16)
        g3 = (rem - g2.astype(F32)).astype(BF16)
        z = (jnp.dot(sums, g1, preferred_element_type=F32)
             + jnp.dot(sums, g2, preferred_element_type=F32)
             + jnp.dot(sums, g3, preferred_element_type=F32))
        e = jnp.exp(z)
        e_b = e[0:c]
        e_tail = e[(GLA_LEVELS + 1) * c:(GLA_LEVELS + 2) * c]
        e_last = e_b[c - 1:c]

        att = jnp.zeros((c, GLA_HEADS * c), F32)
        for level in range(GLA_LEVELS + 1):
            if level < GLA_LEVELS:
                f = e[(level + 1) * c:(level + 2) * c]
                qf, kf = q * f, k * f
            else:
                qf, kf = q, k
            kst = stack_heads(kf, lane_k, GLA_HEADS)
            a = lax.dot_general(qf.astype(BF16), kst, (((1,), (1,)), ((), ())),
                                preferred_element_type=F32)
            att = att + jnp.where(pmask_ref[level] > 0.5, a, 0.0)

        st = state_ref[...]
        vst = stack_heads(v, lane_v, GLA_HEADS)
        o = (lax.dot_general((q * e_b).astype(BF16), st.astype(BF16), (((1,), (1,)), ((), ())),
                             preferred_element_type=F32)
             + jnp.dot(att.astype(BF16), vst, preferred_element_type=F32))
        kv = jnp.dot(v.T.astype(BF16), (k * e_tail).astype(BF16), preferred_element_type=F32)
        state_ref[...] = st * e_last + kv * smask

        hi, lo = _split_bf16(o * o)
        ms = (jnp.dot(hi, hones, preferred_element_type=F32)
              + jnp.dot(lo, hones, preferred_element_type=F32)) * (1.0 / GLA_DV)
        y = o * lax.rsqrt(ms + RMS_EPS) * norm
        o_ref[0, pl.ds(r0, c), :] = y * (gg * _sigmoid(gg))
        return 0

    lax.fori_loop(0, n_chunks, chunk, 0)

    @pl.when(tb == pl.num_programs(1) - 1)
    def _():
        sT_ref[0] = state_ref[...]


def _gla_state_to_kernel(s):
    b = s.shape[0]
    st = jnp.swapaxes(s, 2, 3)
    eye = jnp.eye(GLA_HEADS, dtype=s.dtype)
    full = st[:, :, :, None, :] * eye[None, :, None, :, None]
    return full.reshape(b, GLA_V_WIDTH, GLA_K_WIDTH)


def _gla_state_from_kernel(st):
    b = st.shape[0]
    full = st.reshape(b, GLA_HEADS, GLA_DV, GLA_HEADS, GLA_DK)
    diag = jnp.stack([full[:, h, :, h, :] for h in range(GLA_HEADS)], axis=1)
    return jnp.swapaxes(diag, 2, 3)


def _gla(gla3d, s0t, wg_bf, bg, norm, valid_len=None):
    batch, lp, _ = gla3d.shape
    tb = min(512, lp)
    assert lp % tb == 0 and tb % GLA_CHUNK == 0
    sums, pmasks, smask, hones = _gla_constants()
    const2 = lambda b, t: (0, 0)
    est = 2 * (tb * (GLA_COLS + GLA_V_WIDTH) * 4) + 4 * 1024 * 1024
    return pl.pallas_call(
        functools.partial(_gla_kernel, n_chunks=tb // GLA_CHUNK, valid_len=valid_len),
        grid=(batch, lp // tb),
        in_specs=[pl.BlockSpec((1, tb, GLA_COLS), lambda b, t: (b, t, 0)),
                  pl.BlockSpec((1, GLA_V_WIDTH, GLA_K_WIDTH), lambda b, t: (b, 0, 0)),
                  pl.BlockSpec(sums.shape, const2),
                  pl.BlockSpec(pmasks.shape, lambda b, t: (0, 0, 0)),
                  pl.BlockSpec(smask.shape, const2),
                  pl.BlockSpec(hones.shape, const2),
                  pl.BlockSpec((GLR_PAD, GLA_K_WIDTH), const2),
                  pl.BlockSpec((1, GLA_K_WIDTH), const2),
                  pl.BlockSpec((1, GLA_V_WIDTH), const2)],
        out_specs=(pl.BlockSpec((1, tb, GLA_V_WIDTH), lambda b, t: (b, t, 0)),
                   pl.BlockSpec((1, GLA_V_WIDTH, GLA_K_WIDTH), lambda b, t: (b, 0, 0))),
        out_shape=(jax.ShapeDtypeStruct((batch, lp, GLA_V_WIDTH), F32),
                   jax.ShapeDtypeStruct((batch, GLA_V_WIDTH, GLA_K_WIDTH), F32)),
        scratch_shapes=[pltpu.VMEM((GLA_V_WIDTH, GLA_K_WIDTH), F32)],
        compiler_params=pltpu.CompilerParams(
            dimension_semantics=("parallel", "arbitrary"), vmem_limit_bytes=_vmem_limit(est)),
        name="gla",
    )(gla3d, s0t, jnp.asarray(sums, BF16), jnp.asarray(pmasks), jnp.asarray(smask),
      jnp.asarray(hones, BF16), wg_bf, bg.reshape(1, GLA_K_WIDTH),
      jnp.tile(norm, GLA_HEADS).reshape(1, GLA_V_WIDTH))


def _gelu_tanh(x):
    return 0.5 * x * (1.0 + jnp.tanh(math.sqrt(2.0 / math.pi) * (x + 0.044715 * (x * x * x))))


def _lru_gates(xc, wa_ref, ba_ref, wx_ref, bx_ref, lam_ref):
    xb = xc.astype(BF16)
    r = _sigmoid(jnp.dot(xb, wa_ref[...], preferred_element_type=F32) + ba_ref[...])
    i = _sigmoid(jnp.dot(xb, wx_ref[...], preferred_element_type=F32) + bx_ref[...])
    log_a = LRU_C * r * _log_sigmoid(lam_ref[...])
    a = jnp.exp(log_a)
    one_minus_a2 = -jnp.tanh(log_a) * (a * a + 1.0)
    u = jnp.sqrt(one_minus_a2) * (i * xc)
    return a, u


def _shift_rows(x, d, fill):
    rows = lax.broadcasted_iota(jnp.int32, x.shape, 0)
    return jnp.where(rows >= d, pltpu.roll(x, d, axis=0), fill)


def _lru_prompt_kernel(x_ref, cw_ref, cb_ref, wa_ref, ba_ref, wx_ref, bx_ref, lam_ref, norm_ref,
                       o_ref, conv_ref, h_ref, *, seq):
    x = x_ref[0, :, 0:LRU_WIDTH]
    gate = x_ref[0, :, LRU_WIDTH:2 * LRU_WIDTH]
    xc = x * cw_ref[CONV_W - 1:CONV_W, :] + cb_ref[...]
    for j in range(1, CONV_W):
        xc = xc + _shift_rows(x, j, 0.0) * cw_ref[CONV_W - 1 - j:CONV_W - j, :]
    a, u = _lru_gates(xc, wa_ref, ba_ref, wx_ref, bx_ref, lam_ref)
    d = 1
    while d < seq:
        u = a * _shift_rows(u, d, 0.0) + u
        if 2 * d < seq:
            a = a * _shift_rows(a, d, 1.0)
        d *= 2
    o_ref[0] = _rms(u * _gelu_tanh(gate), norm_ref[...])
    conv_ref[0] = x[seq - (CONV_W - 1):seq, :]
    h_ref[0] = u[seq - 1:seq, :]


def _lru_prompt(lru3d, conv_w, conv_b, wa_bf, ba, wx_bf, bx, lam, norm):
    batch, seq, _ = lru3d.shape
    assert seq >= CONV_W - 1
    vec = lambda a: a.reshape(1, LRU_WIDTH)
    const = lambda b: (0, 0)
    vspec = pl.BlockSpec((1, LRU_WIDTH), const)
    wspec = pl.BlockSpec((LRU_WIDTH, LRU_WIDTH), const)
    est = 12 * seq * LRU_WIDTH * 4
    return pl.pallas_call(
        functools.partial(_lru_prompt_kernel, seq=seq),
        grid=(batch,),
        in_specs=[pl.BlockSpec((1, seq, LRU_COLS), lambda b: (b, 0, 0)),
                  pl.BlockSpec((CONV_W, LRU_WIDTH), const), vspec,
                  wspec, vspec, wspec, vspec, vspec, vspec],
        out_specs=(pl.BlockSpec((1, seq, LRU_WIDTH), lambda b: (b, 0, 0)),
                   pl.BlockSpec((1, CONV_W - 1, LRU_WIDTH), lambda b: (b, 0, 0)),
                   pl.BlockSpec((1, 1, LRU_WIDTH), lambda b: (b, 0, 0))),
        out_shape=(jax.ShapeDtypeStruct((batch, seq, LRU_WIDTH), F32),
                   jax.ShapeDtypeStruct((batch, CONV_W - 1, LRU_WIDTH), F32),
                   jax.ShapeDtypeStruct((batch, 1, LRU_WIDTH), F32)),
        compiler_params=pltpu.CompilerParams(
            dimension_semantics=("parallel",), vmem_limit_bytes=_vmem_limit(est)),
        name="lru_prompt",
    )(lru3d, conv_w, vec(conv_b), wa_bf, vec(ba), wx_bf, vec(bx), vec(lam), vec(norm))


def _lru_step_kernel(x_ref, buf_ref, h0_ref, cw_ref, cb_ref, wa_ref, ba_ref, wx_ref, bx_ref, lam_ref,
                     norm_ref, o_ref, conv_ref, h_ref):
    w = LRU_WIDTH
    x = x_ref[:, 0:w]
    gate = x_ref[:, w:2 * w]
    xc = x * cw_ref[CONV_W - 1:CONV_W, :] + cb_ref[...]
    for j in range(CONV_W - 1):
        xc = xc + buf_ref[:, j * w:(j + 1) * w] * cw_ref[j:j + 1, :]
    a, u = _lru_gates(xc, wa_ref, ba_ref, wx_ref, bx_ref, lam_ref)
    h = a * h0_ref[...] + u
    o_ref[...] = _rms(h * _gelu_tanh(gate), norm_ref[...])
    for j in range(CONV_W - 2):
        conv_ref[:, j * w:(j + 1) * w] = buf_ref[:, (j + 1) * w:(j + 2) * w]
    conv_ref[:, (CONV_W - 2) * w:(CONV_W - 1) * w] = x
    h_ref[...] = h


def _lru_step(lru2d, conv_buf, h0, conv_w, conv_b, wa_bf, ba, wx_bf, bx, lam, norm):
    batch = lru2d.shape[0]
    vec = lambda a: a.reshape(1, LRU_WIDTH)
    nbuf = (CONV_W - 1) * LRU_WIDTH
    o, conv, h = pl.pallas_call(
        _lru_step_kernel,
        out_shape=(jax.ShapeDtypeStruct((batch, LRU_WIDTH), F32),
                   jax.ShapeDtypeStruct((batch, nbuf), F32),
                   jax.ShapeDtypeStruct((batch, LRU_WIDTH), F32)),
        name="lru_step",
    )(lru2d, conv_buf.reshape(batch, nbuf), h0, conv_w, vec(conv_b), wa_bf, vec(ba), wx_bf, vec(bx),
      vec(lam), vec(norm))
    return o, conv.reshape(batch, CONV_W - 1, LRU_WIDTH), h


FF_CHUNK = 1024


def _mix_mlp_kernel(x_ref, sb_ref, gla_ref, lru_ref, sbn_ref, wout_ref, ln2_ref, wup_ref, wdn_ref,
                    lnf_ref, y_ref, *, final_norm):
    sb = _rms(sb_ref[...], sbn_ref[...]).astype(BF16)
    o1, o2 = SB_WIDTH, SB_WIDTH + GLA_V_WIDTH
    mix = (jnp.dot(sb, wout_ref[0:o1, :], preferred_element_type=F32)
           + jnp.dot(gla_ref[...].astype(BF16), wout_ref[o1:o2, :], preferred_element_type=F32)
           + jnp.dot(lru_ref[...].astype(BF16), wout_ref[o2:D_MODEL, :], preferred_element_type=F32))
    x1 = x_ref[...] + mix
    hm = _rms(x1, ln2_ref[...]).astype(BF16)
    acc = x1
    for c in range(D_FF // FF_CHUNK):
        up = jnp.dot(hm, wup_ref[:, c * FF_CHUNK:(c + 1) * FF_CHUNK], preferred_element_type=F32)
        act = jnp.square(jnp.maximum(up, 0.0)).astype(BF16)
        acc = acc + jnp.dot(act, wdn_ref[c * FF_CHUNK:(c + 1) * FF_CHUNK, :],
                            preferred_element_type=F32)
    y_ref[...] = _rms(acc, lnf_ref[...]) if final_norm else acc


def _mix_mlp(x2d, o_sb, o_gla, o_lru, sb_norm, wout_bf, ln2, wup_bf, wdn_bf, ln_f, final_norm):
    t = x2d.shape[0]
    tm = min(512, t)
    assert t % tm == 0
    row = lambda i: (i, 0)
    const = lambda i: (0, 0)
    single = pl.Buffered(1)
    est = ((D_MODEL * D_MODEL + 2 * D_MODEL * D_FF) * 2
           + 2 * tm * (2 * D_MODEL + SB_WIDTH + GLA_V_WIDTH + LRU_WIDTH) * 4
           + 4 * tm * FF_CHUNK * 4)
    return pl.pallas_call(
        functools.partial(_mix_mlp_kernel, final_norm=final_norm),
        grid=(t // tm,),
        in_specs=[pl.BlockSpec((tm, D_MODEL), row),
                  pl.BlockSpec((tm, SB_WIDTH), row),
                  pl.BlockSpec((tm, GLA_V_WIDTH), row),
                  pl.BlockSpec((tm, LRU_WIDTH), row),
                  pl.BlockSpec((1, SB_WIDTH), const),
                  pl.BlockSpec((D_MODEL, D_MODEL), const, pipeline_mode=single),
                  pl.BlockSpec((1, D_MODEL), const),
                  pl.BlockSpec((D_MODEL, D_FF), const, pipeline_mode=single),
                  pl.BlockSpec((D_FF, D_MODEL), const, pipeline_mode=single),
                  pl.BlockSpec((1, D_MODEL), const)],
        out_specs=pl.BlockSpec((tm, D_MODEL), row),
        out_shape=jax.ShapeDtypeStruct((t, D_MODEL), F32),
        compiler_params=pltpu.CompilerParams(
            dimension_semantics=("parallel",), vmem_limit_bytes=_vmem_limit(est)),
        name="mix_mlp",
    )(x2d, o_sb, o_gla, o_lru, sb_norm.reshape(1, SB_WIDTH), wout_bf, ln2.reshape(1, D_MODEL),
      wup_bf, wdn_bf, ln_f.reshape(1, D_MODEL))


def _block_diag(w):
    n, c, d = w.shape
    eye = jnp.eye(n, dtype=w.dtype)
    return (w[:, :, None, :] * eye[:, None, :, None]).reshape(n * c, n * d)


def _prep_layer(l, w):
    w_in = w["w_in"][l]
    split = SB_COLS + 2 * GLA_K_WIDTH + 2 * GLA_V_WIDTH + GLA_RANK
    w_in_p = jnp.concatenate(
        [w_in[:, :split], jnp.zeros((D_MODEL, GLR_PAD - GLA_RANK), w_in.dtype), w_in[:, split:]], axis=1)
    wg = jnp.concatenate(
        [w["gla_wg2"][l], jnp.zeros((GLR_PAD - GLA_RANK, GLA_K_WIDTH), F32)], axis=0)
    wkvt = jnp.transpose(w_in)[SB_WIDTH:3 * SB_WIDTH]
    return dict(
        ln1=w["ln1"][l], w_in=w_in_p.astype(BF16), wkvt=wkvt.astype(BF16),
        sb_bias=w["sb_bias"][l], sb_norm=w["sb_norm"][l],
        wg=wg.astype(BF16), bg=w["gla_bg"][l], gla_norm=w["gla_norm"][l],
        conv_w=w["conv_w"][l], conv_b=w["conv_b"][l],
        wa=_block_diag(w["lru_wa"][l]).astype(BF16), ba=w["lru_ba"][l],
        wx=_block_diag(w["lru_wx"][l]).astype(BF16), bx=w["lru_bx"][l],
        lam=w["lru_lambda"][l], lru_norm=w["lru_norm"][l],
        w_out=w["w_out"][l].astype(BF16), ln2=w["ln2"][l],
        w_up=w["w_up"][l].astype(BF16), w_down=w["w_down"][l].astype(BF16))


def _prompt_trunk(x, layers, ln_f):
    batch, seq, _ = x.shape
    depth = len(layers)
    t = batch * seq
    x2d = x.reshape(t, D_MODEL)
    ks, vs, ss, cs, hs = [], [], [], [], []
    s0t = jnp.zeros((batch, GLA_V_WIDTH, GLA_K_WIDTH), F32)
    lp = -(-seq // GLA_CHUNK) * GLA_CHUNK
    key_tile = min(SB_TILE, seq)

    def heads_last(xt):
        return jnp.transpose(xt.reshape(batch, SB_HEADS, SB_HEAD_DIM, seq), (0, 3, 1, 2))

    for l, p in enumerate(layers):
        qb, kt, vt, ktb, vtb, gla3d, lru3d = _in_proj_prompt(
            x2d.reshape(batch, seq, D_MODEL), p["ln1"], p["w_in"], p["wkvt"], key_tile)
        o_sb = _sb_attention_prompt(qb, ktb, vtb, p["sb_bias"]).reshape(t, SB_WIDTH)
        if lp != seq:
            gla3d = jnp.pad(gla3d, ((0, 0), (0, lp - seq), (0, 0)))
        o_gla, st = _gla(gla3d, s0t, p["wg"], p["bg"], p["gla_norm"],
                         valid_len=None if lp == seq else seq)
        o_gla = o_gla[:, :seq].reshape(t, GLA_V_WIDTH)
        o_lru, conv, h_last = _lru_prompt(lru3d, p["conv_w"], p["conv_b"],
                                          p["wa"], p["ba"], p["wx"], p["bx"], p["lam"], p["lru_norm"])
        x2d = _mix_mlp(x2d, o_sb, o_gla, o_lru.reshape(t, LRU_WIDTH), p["sb_norm"], p["w_out"],
                       p["ln2"], p["w_up"], p["w_down"], ln_f, final_norm=(l == depth - 1))
        ks.append(heads_last(kt))
        vs.append(heads_last(vt))
        ss.append(_gla_state_from_kernel(st))
        cs.append(conv)
        hs.append(h_last.reshape(batch, LRU_WIDTH))
    return (x2d.reshape(batch, seq, D_MODEL), jnp.stack(ks), jnp.stack(vs), jnp.stack(ss),
            jnp.stack(cs), jnp.stack(hs))


def _sample_trunk(x, cache_k, cache_v, page_table, gla_s0, conv0, lru0, layers, ln_f):
    batch, seq, _ = x.shape
    assert seq == 1
    depth = len(layers)
    x2d = x.reshape(batch, D_MODEL)
    ks, vs, ss, cs, hs = [], [], [], [], []
    for l, p in enumerate(layers):
        qb, k, v, gla, lru = _in_proj_step(x2d, p["ln1"], p["w_in"])
        o_sb = _sb_attention_decode(qb, cache_k, cache_v, page_table, p["sb_bias"], l)
        gla3d = jnp.pad(gla.reshape(batch, 1, GLA_COLS), ((0, 0), (0, GLA_CHUNK - 1), (0, 0)))
        o_gla, st = _gla(gla3d, _gla_state_to_kernel(gla_s0[l]), p["wg"], p["bg"], p["gla_norm"],
                         valid_len=1)
        o_lru, conv, h_last = _lru_step(lru, conv0[l], lru0[l], p["conv_w"], p["conv_b"], p["wa"],
                                        p["ba"], p["wx"], p["bx"], p["lam"], p["lru_norm"])
        x2d = _mix_mlp(x2d, o_sb, o_gla[:, 0], o_lru, p["sb_norm"], p["w_out"], p["ln2"], p["w_up"],
                       p["w_down"], ln_f, final_norm=(l == depth - 1))
        ks.append(k.reshape(batch, 1, SB_HEADS, SB_HEAD_DIM))
        vs.append(v.reshape(batch, 1, SB_HEADS, SB_HEAD_DIM))
        ss.append(_gla_state_from_kernel(st))
        cs.append(conv)
        hs.append(h_last)
    return (x2d.reshape(batch, 1, D_MODEL), jnp.stack(ks), jnp.stack(vs), jnp.stack(ss),
            jnp.stack(cs), jnp.stack(hs))


def kernel(x_prompt, x_sample, cache_sb_k, cache_sb_v, page_table, state_gla, state_conv, state_lru, ln1, w_in, sb_bias, sb_norm, gla_wg2, gla_bg, gla_norm, conv_w, conv_b, lru_wa, lru_ba, lru_wx, lru_bx, lru_lambda, lru_norm, w_out, ln2, w_up, w_down, ln_f):
    w = dict(ln1=ln1, w_in=w_in, sb_bias=sb_bias, sb_norm=sb_norm, gla_wg2=gla_wg2, gla_bg=gla_bg,
             gla_norm=gla_norm, conv_w=conv_w, conv_b=conv_b, lru_wa=lru_wa, lru_ba=lru_ba,
             lru_wx=lru_wx, lru_bx=lru_bx, lru_lambda=lru_lambda, lru_norm=lru_norm, w_out=w_out,
             ln2=ln2, w_up=w_up, w_down=w_down)
    layers = [_prep_layer(l, w) for l in range(ln1.shape[0])]
    y_p, k_p, v_p, gla_p, conv_p, lru_p = _prompt_trunk(x_prompt, layers, ln_f)
    y_s, k_s, v_s, gla_s, conv_s, lru_s = _sample_trunk(
        x_sample, cache_sb_k, cache_sb_v, page_table, state_gla, state_conv, state_lru, layers, ln_f)
    return (y_p, y_s, k_p, v_p, k_s, v_s, gla_p, gla_s, conv_p, conv_s, lru_p, lru_s)
```

```python
import functools
import math

import numpy as np
import jax
import jax.numpy as jnp
from jax import lax
from jax.experimental import pallas as pl
from jax.experimental.pallas import tpu as pltpu

F32 = jnp.float32
BF16 = jnp.bfloat16

LANES_V7X = 128
VMEM_BYTES_V7X = 64 * 1024 * 1024

D_MODEL = 1024
SB_HEAD_DIM = 64
SB_WIDTH = D_MODEL // 2
SB_HEADS = SB_WIDTH // SB_HEAD_DIM
GLA_HEADS = 4
GLA_V_WIDTH = D_MODEL // 4
GLA_DV = GLA_V_WIDTH // GLA_HEADS
GLA_DK = GLA_DV // 2
GLA_K_WIDTH = GLA_HEADS * GLA_DK
GLA_RANK = 16
GLA_TAU = 16.0
GLA_CHUNK = 64
LRU_WIDTH = D_MODEL // 4
LRU_BLOCKS = 4
LRU_C = 8.0
CONV_W = 4
D_FF = 4 * D_MODEL
RMS_EPS = 1e-6

GLR_PAD = LANES_V7X
SB_COLS = 3 * SB_WIDTH
GLA_COLS = 2 * GLA_K_WIDTH + 2 * GLA_V_WIDTH + GLR_PAD
LRU_COLS = 2 * LRU_WIDTH
IN_COLS = SB_COLS + GLA_COLS + LRU_COLS


def _vmem_limit(nbytes):
    return int(min(max(2 * nbytes, 32 * 1024 * 1024), VMEM_BYTES_V7X - 8 * 1024 * 1024))


def _rms(x, g):
    ms = jnp.mean(x * x, axis=-1, keepdims=True)
    return x * lax.rsqrt(ms + RMS_EPS) * g


def _split_bf16(x):
    hi = x.astype(BF16)
    lo = (x - hi.astype(F32)).astype(BF16)
    return hi, lo


def _log_sigmoid(x):
    return jnp.minimum(x, 0.0) - jnp.log(1.0 + jnp.exp(-jnp.abs(x)))


def _sigmoid(x):
    return 1.0 / (1.0 + jnp.exp(-x))


SB_TILE = 256


def _proj_common(h, w_ref, qb_ref, gla_ref, lru_ref):
    def proj(lo, hi):
        return jnp.dot(h, w_ref[:, lo:hi], preferred_element_type=F32)

    qb_ref[...] = (proj(0, SB_WIDTH) * (SB_HEAD_DIM ** -0.5)).astype(qb_ref.dtype).reshape(qb_ref.shape)
    o = SB_COLS
    gq = proj(o, o + GLA_K_WIDTH) * (GLA_DK ** -0.5)
    rest = proj(o + GLA_K_WIDTH, o + GLA_COLS)
    gla_ref[...] = jnp.concatenate([gq, rest], axis=1).reshape(gla_ref.shape)
    o = SB_COLS + GLA_COLS
    lru_ref[...] = proj(o, o + LRU_COLS).reshape(lru_ref.shape)
    return proj


def _in_proj_prompt_kernel(x_ref, g_ref, w_ref, wkvt_ref, qb_ref, kt_ref, vt_ref, ktb_ref, vtb_ref,
                           gla_ref, lru_ref, *, key_tile):
    h = _rms(x_ref[0], g_ref[...]).astype(BF16)
    _proj_common(h, w_ref, qb_ref, gla_ref, lru_ref)
    for t_ref, tb_ref, r0 in ((kt_ref, ktb_ref, 0), (vt_ref, vtb_ref, SB_WIDTH)):
        xt = lax.dot_general(wkvt_ref[r0:r0 + SB_WIDTH, :], h, (((1,), (1,)), ((), ())),
                             preferred_element_type=F32)
        t_ref[0] = xt
        for c in range(xt.shape[1] // key_tile):
            tb_ref[0, c] = xt[:, c * key_tile:(c + 1) * key_tile].astype(BF16)


def _in_proj_prompt(x, ln, w_bf, wkvt_bf, key_tile):
    batch, seq, _ = x.shape
    tm = min(512, seq)
    assert seq % tm == 0 and tm % key_tile == 0
    nt = seq // tm
    const = lambda b, i: (0, 0)
    tok = lambda b, i: (b, i, 0)
    out_shapes = (
        jax.ShapeDtypeStruct((batch, seq, SB_WIDTH), BF16),
        jax.ShapeDtypeStruct((batch, SB_WIDTH, seq), F32),
        jax.ShapeDtypeStruct((batch, SB_WIDTH, seq), F32),
        jax.ShapeDtypeStruct((batch, seq // key_tile, SB_WIDTH, key_tile), BF16),
        jax.ShapeDtypeStruct((batch, seq // key_tile, SB_WIDTH, key_tile), BF16),
        jax.ShapeDtypeStruct((batch, seq, GLA_COLS), F32),
        jax.ShapeDtypeStruct((batch, seq, LRU_COLS), F32),
    )
    tspec = pl.BlockSpec((1, SB_WIDTH, tm), lambda b, i: (b, 0, i))
    tbspec = pl.BlockSpec((1, tm // key_tile, SB_WIDTH, key_tile), lambda b, i: (b, i, 0, 0))
    out_specs = (pl.BlockSpec((1, tm, SB_WIDTH), tok), tspec, tspec, tbspec, tbspec,
                 pl.BlockSpec((1, tm, GLA_COLS), tok), pl.BlockSpec((1, tm, LRU_COLS), tok))
    est = 2 * (tm * D_MODEL * 4 + D_MODEL * (IN_COLS + 2 * SB_WIDTH) * 2
               + tm * (SB_WIDTH * (2 + 8 + 4) + (GLA_COLS + LRU_COLS) * 4))
    return pl.pallas_call(
        functools.partial(_in_proj_prompt_kernel, key_tile=key_tile),
        grid=(batch, nt),
        in_specs=[pl.BlockSpec((1, tm, D_MODEL), tok),
                  pl.BlockSpec((1, D_MODEL), const),
                  pl.BlockSpec((D_MODEL, IN_COLS), const),
                  pl.BlockSpec((2 * SB_WIDTH, D_MODEL), const)],
        out_specs=out_specs,
        out_shape=out_shapes,
        compiler_params=pltpu.CompilerParams(
            dimension_semantics=("parallel", "parallel"), vmem_limit_bytes=_vmem_limit(est)),
        name="in_proj_prompt",
    )(x, ln.reshape(1, D_MODEL), w_bf, wkvt_bf)


def _in_proj_step_kernel(x_ref, g_ref, w_ref, qb_ref, k_ref, v_ref, gla_ref, lru_ref):
    h = _rms(x_ref[...], g_ref[...]).astype(BF16)
    proj = _proj_common(h, w_ref, qb_ref, gla_ref, lru_ref)
    k_ref[...] = proj(SB_WIDTH, 2 * SB_WIDTH)
    v_ref[...] = proj(2 * SB_WIDTH, 3 * SB_WIDTH)


def _in_proj_step(x2d, ln, w_bf):
    t = x2d.shape[0]
    out_shapes = (
        jax.ShapeDtypeStruct((t, SB_WIDTH), F32),
        jax.ShapeDtypeStruct((t, SB_WIDTH), F32),
        jax.ShapeDtypeStruct((t, SB_WIDTH), F32),
        jax.ShapeDtypeStruct((t, GLA_COLS), F32),
        jax.ShapeDtypeStruct((t, LRU_COLS), F32),
    )
    return pl.pallas_call(
        _in_proj_step_kernel,
        out_shape=out_shapes,
        compiler_params=pltpu.CompilerParams(
            vmem_limit_bytes=_vmem_limit(D_MODEL * IN_COLS * 2)),
        name="in_proj_step",
    )(x2d, ln.reshape(1, D_MODEL), w_bf)


def _softplus(z):
    return jnp.maximum(z, 0.0) + jnp.log(1.0 + jnp.exp(-jnp.abs(z)))


SB_MASKED = -1e30


def _sb_attn_kernel(bias_ref, q_ref, k_ref, v_ref, o_ref, sp_ref, zs_ref, sp0_ref, acc_ref, *, tile):
    hp = pl.program_id(1)
    qi = pl.program_id(2)
    q = q_ref[0].astype(F32)
    lane = lax.broadcasted_iota(jnp.int32, (tile, LANES_V7X), 1)
    first_head = lane < SB_HEAD_DIM
    qs = (jnp.where(first_head, q, 0.0).astype(BF16), jnp.where(first_head, 0.0, q).astype(BF16))
    biases = (bias_ref[2 * hp], bias_ref[2 * hp + 1])
    r_i = lax.broadcasted_iota(jnp.int32, (tile, tile), 0)
    c_i = lax.broadcasted_iota(jnp.int32, (tile, tile), 1)
    upper = jnp.where(r_i > c_i, 1.0, 0.0).astype(BF16)
    causal = c_i < r_i
    acc_ref[...] = jnp.zeros_like(acc_ref)

    def scores(kt, slot, mask):
        for h in range(2):
            z = jnp.dot(qs[h], kt, preferred_element_type=F32) + biases[h]
            sp = _softplus(z)
            zs = z - sp
            if mask is not None:
                sp = jnp.where(mask, sp, 0.0)
                zs = jnp.where(mask, zs, SB_MASKED)
            sp_ref[slot, h] = sp.astype(BF16)
            zs_ref[slot, h] = zs
            sp0_ref[slot, h] = sp[:, 0:1]

    def weigh(vt, slot, laters):
        out = []
        for h in range(2):
            excl = jnp.dot(sp_ref[slot, h], upper, preferred_element_type=F32)
            w = jnp.exp(zs_ref[slot, h] - excl - laters[h])
            acc_ref[h] += lax.dot_general(w.astype(BF16), vt, (((1,), (1,)), ((), ())),
                                          preferred_element_type=F32)
            out.append(laters[h] + excl[:, 0:1] + sp0_ref[slot, h])
        return tuple(out)

    scores(k_ref[0, qi], 0, causal)

    def pair(m, laters):
        j = qi - 1 - 2 * m
        scores(k_ref[0, j], 1, None)
        laters = weigh(v_ref[0, j + 1], 0, laters)
        scores(k_ref[0, j - 1], 0, None)
        return weigh(v_ref[0, j], 1, laters)

    zero = jnp.zeros((tile, 1), F32)
    laters = lax.fori_loop(0, qi // 2, pair, (zero, zero))

    @pl.when(qi % 2 == 1)
    def _():
        scores(k_ref[0, 0], 1, None)
        weigh(v_ref[0, 0], 1, weigh(v_ref[0, 1], 0, laters))

    @pl.when(qi % 2 == 0)
    def _():
        weigh(v_ref[0, 0], 0, laters)

    o_ref[0] = jnp.where(first_head, acc_ref[0], acc_ref[1])


def _sb_attention_prompt(qb, ktb, vtb, sb_bias):
    batch, seq, _ = qb.shape
    nq, tile = ktb.shape[1], ktb.shape[3]
    n_pairs = SB_WIDTH // LANES_V7X
    est = 2 * (2 * seq * LANES_V7X * 2 + tile * LANES_V7X * 6) + 24 * tile * tile * 4
    kv_spec = pl.BlockSpec((1, nq, LANES_V7X, tile), lambda b, hp, qi: (b, 0, hp, 0))
    return pl.pallas_call(
        functools.partial(_sb_attn_kernel, tile=tile),
        grid=(batch, n_pairs, nq),
        in_specs=[pl.BlockSpec(memory_space=pltpu.SMEM),
                  pl.BlockSpec((1, tile, LANES_V7X), lambda b, hp, qi: (b, qi, hp)),
                  kv_spec, kv_spec],
        out_specs=pl.BlockSpec((1, tile, LANES_V7X), lambda b, hp, qi: (b, qi, hp)),
        out_shape=jax.ShapeDtypeStruct((batch, seq, SB_WIDTH), F32),
        scratch_shapes=[pltpu.VMEM((2, 2, tile, tile), BF16),
                        pltpu.VMEM((2, 2, tile, tile), F32),
                        pltpu.VMEM((2, 2, tile, 1), F32),
                        pltpu.VMEM((2, tile, LANES_V7X), F32)],
        compiler_params=pltpu.CompilerParams(
            dimension_semantics=("parallel", "parallel", "parallel"),
            vmem_limit_bytes=_vmem_limit(est)),
        name="sb_attn_prompt",
    )(sb_bias, qb, ktb, vtb)


def _sb_decode_kernel(pt_ref, q_ref, bias_ref, *refs, pages_per_step, page):
    k_refs = refs[:pages_per_step]
    v_refs = refs[pages_per_step:2 * pages_per_step]
    o_ref = refs[2 * pages_per_step]
    acc_ref, later_ref = refs[2 * pages_per_step + 1:]
    step = pl.program_id(1)

    @pl.when(step == 0)
    def _():
        acc_ref[...] = jnp.zeros_like(acc_ref)
        later_ref[...] = jnp.zeros_like(later_ref)

    q = q_ref[0]
    bias = bias_ref[...]
    r_i = lax.broadcasted_iota(jnp.int32, (page, page), 0)
    c_i = lax.broadcasted_iota(jnp.int32, (page, page), 1)
    upper = jnp.where(r_i > c_i, 1.0, 0.0).astype(BF16)
    grouped = (SB_HEADS, SB_HEAD_DIM, page)

    later = later_ref[...]
    weights = []
    for i in range(pages_per_step):
        z = jnp.sum((k_refs[i][0] * q).reshape(grouped), axis=1) + bias
        sp = _softplus(z)
        hi, lo = _split_bf16(sp)
        excl = (jnp.dot(hi, upper, preferred_element_type=F32)
                + jnp.dot(lo, upper, preferred_element_type=F32))
        weights.append(jnp.exp(z - sp - excl - later))
        later = later + excl[:, 0:1] + sp[:, 0:1]
    later_ref[...] = later

    for h in range(SB_HEADS):
        rows = slice(h * SB_HEAD_DIM, (h + 1) * SB_HEAD_DIM)
        part = acc_ref[rows, :]
        for i in range(pages_per_step):
            part = part + v_refs[i][0, rows, :] * weights[i][h:h + 1, :]
        acc_ref[rows, :] = part

    @pl.when(step == pl.num_programs(1) - 1)
    def _():
        o_ref[0] = jnp.sum(acc_ref[...], axis=1, keepdims=True)


def _sb_attention_decode(q, cache_k, cache_v, page_table, sb_bias, layer):
    batch, n_pages = page_table.shape
    depth, n_pool, page = cache_k.shape[:3]
    pages_per_step = math.gcd(n_pages, 8)
    n_steps = n_pages // pages_per_step
    ck = jnp.transpose(cache_k, (0, 1, 3, 4, 2)).reshape(depth * n_pool, SB_WIDTH, page)
    cv = jnp.transpose(cache_v, (0, 1, 3, 4, 2)).reshape(depth * n_pool, SB_WIDTH, page)
    base = layer * n_pool

    def page_spec(i):
        def index_map(b, s, pt):
            logical = n_pages - 1 - (s * pages_per_step + i)
            return (base + pt[b * n_pages + logical], 0, 0)
        return pl.BlockSpec((1, SB_WIDTH, page), index_map)

    grid_spec = pltpu.PrefetchScalarGridSpec(
        num_scalar_prefetch=1,
        grid=(batch, n_steps),
        in_specs=[pl.BlockSpec((1, SB_WIDTH, page), lambda b, s, pt: (b, 0, 0)),
                  pl.BlockSpec((SB_HEADS, page), lambda b, s, pt: (0, 0))]
                 + [page_spec(i) for i in range(pages_per_step)] * 2,
        out_specs=pl.BlockSpec((1, SB_WIDTH, 1), lambda b, s, pt: (b, 0, 0)),
        scratch_shapes=[pltpu.VMEM((SB_WIDTH, page), F32), pltpu.VMEM((SB_HEADS, 1), F32)],
    )
    est = 2 * (2 * pages_per_step + 2) * page * SB_WIDTH * 4
    out = pl.pallas_call(
        functools.partial(_sb_decode_kernel, pages_per_step=pages_per_step, page=page),
        grid_spec=grid_spec,
        out_shape=jax.ShapeDtypeStruct((batch, SB_WIDTH, 1), F32),
        compiler_params=pltpu.CompilerParams(
            dimension_semantics=("parallel", "arbitrary"), vmem_limit_bytes=_vmem_limit(est)),
        name="sb_attn_decode",
    )(page_table.reshape(-1), jnp.broadcast_to(q[:, :, None], (batch, SB_WIDTH, page)),
      jnp.broadcast_to(sb_bias[:, None], (SB_HEADS, page)),
      *([ck] * pages_per_step), *([cv] * pages_per_step))
    return out.reshape(batch, SB_WIDTH)


GLA_LEVELS = int(math.log2(GLA_CHUNK))
GLA_MATMUL_BLOCKS = (8, 4)
GLA_BATCH = 4


def _gla_constants():
    c = GLA_CHUNK
    idx = np.arange(c)
    r, j = idx[:, None], idx[None, :]
    blocks = [(j <= r)]
    masks = []
    for level in range(GLA_LEVELS):
        m = c >> level
        half = m // 2
        mid = (r // m) * m + half
        second = r >= mid
        if m in GLA_MATMUL_BLOCKS:
            blocks.append(np.where(second, (j >= mid) & (j <= r), (j > r) & (j < mid)))
        same = (r // m) == (j // m)
        masks.append(same & ((r % m) >= half) & ((j % m) < half))
    masks.append(r == j)
    sums = np.concatenate(blocks, axis=0).astype(np.float32)
    pair_masks = np.stack([np.tile(m, (1, GLA_HEADS)) for m in masks]).astype(np.float32)
    hv = np.arange(GLA_V_WIDTH)[:, None] // GLA_DV
    hk = np.arange(GLA_K_WIDTH)[None, :] // GLA_DK
    state_mask = (hv == hk).astype(np.float32)
    hv2 = np.arange(GLA_V_WIDTH)
    head_ones = (hv2[:, None] // GLA_DV == hv2[None, :] // GLA_DV).astype(np.float32)
    return sums, pair_masks, state_mask, head_ones


def _gla_kernel(x_ref, s0_ref, sums_ref, pmask_ref, smask_ref, hones_ref, wg_ref, bg_ref, norm_ref,
                o_ref, sT_ref, state_ref, *, n_chunks, valid_len):
    c = GLA_CHUNK
    tb = pl.program_id(1)

    @pl.when(tb == 0)
    def _():
        state_ref[...] = s0_ref[...]

    lane_k = lax.broadcasted_iota(jnp.int32, (c, GLA_K_WIDTH), 1) // GLA_DK
    lane_v = lax.broadcasted_iota(jnp.int32, (c, GLA_V_WIDTH), 1) // GLA_DV
    row = lax.broadcasted_iota(jnp.int32, (c, GLA_K_WIDTH), 0)
    sums = sums_ref[...]
    smask = smask_ref[...]
    hones = hones_ref[...]
    wg = wg_ref[...]
    bg = bg_ref[...]
    norm = norm_ref[...]

    def stack_heads(x, lane_head, n):
        return jnp.concatenate(
            [jnp.where(lane_head == h, x, 0.0).astype(BF16) for h in range(n)], axis=0)

    def midpoint_rows(b, m):
        half = m // 2
        return jnp.concatenate(
            [jnp.broadcast_to(b[s + half - 1:s + half, :], (m, GLA_K_WIDTH)) for s in range(0, c, m)],
            axis=0)

    nt_dims = (((1,), (1,)), ((), ()))
    seqs = range(x_ref.shape[0])

    def chunk(ci, _):
        r0 = pl.multiple_of(ci * c, c)
        rows = pl.ds(r0, c)
        o1, o2, o3, o4 = GLA_K_WIDTH, 2 * GLA_K_WIDTH, 2 * GLA_K_WIDTH + GLA_V_WIDTH, GLA_COLS - GLR_PAD
        q = [x_ref[bi, rows, 0:o1] for bi in seqs]
        k = [x_ref[bi, rows, o1:o2] for bi in seqs]
        gate = [jnp.dot(x_ref[bi, rows, o4:GLA_COLS].astype(BF16), wg, preferred_element_type=F32) + bg
                for bi in seqs]
        g = [_log_sigmoid(x) * (1.0 / GLA_TAU) for x in gate]
        if valid_len is not None:
            live = row + tb * (n_chunks * c) + r0 < valid_len
            g = [jnp.where(live, x, 0.0) for x in g]
        gs = [_split_bf16(x) for x in g]
        zz = [jnp.dot(sums, hi, preferred_element_type=F32) + jnp.dot(sums, lo, preferred_element_type=F32)
              for hi, lo in gs]
        b = [x[0:c] for x in zz]
        e_b = [jnp.exp(x) for x in b]
        e_tail = [jnp.exp(x[c - 1:c] - x) for x in b]

        att = [jnp.zeros((c, GLA_HEADS * c), F32) for _ in seqs]
        nmat = 1
        for level in range(GLA_LEVELS + 1):
            m = c >> level
            if level == GLA_LEVELS:
                logf = None
            elif m in GLA_MATMUL_BLOCKS:
                logf = [x[nmat * c:(nmat + 1) * c] for x in zz]
                nmat += 1
            elif m == 2:
                logf = [jnp.where(row % 2 == 1, x, 0.0) for x in g]
            else:
                logf = [-jnp.abs(x - midpoint_rows(x, m)) for x in b]
            if logf is None:
                qf, kf = q, k
            else:
                f = [jnp.exp(x) for x in logf]
                qf = [x * y for x, y in zip(q, f)]
                kf = [x * y for x, y in zip(k, f)]
            kst = [stack_heads(x, lane_k, GLA_HEADS) for x in kf]
            a = [lax.dot_general(x.astype(BF16), y, nt_dims, preferred_element_type=F32)
                 for x, y in zip(qf, kst)]
            pm = pmask_ref[level] > 0.5
            att = [x + jnp.where(pm, y, 0.0) for x, y in zip(att, a)]

        v = [x_ref[bi, rows, o2:o3] for bi in seqs]
        st = [state_ref[bi] for bi in seqs]
        vst = [stack_heads(x, lane_v, GLA_HEADS) for x in v]
        o = [lax.dot_general((x * y).astype(BF16), s.astype(BF16), nt_dims, preferred_element_type=F32)
             + jnp.dot(a.astype(BF16), w, preferred_element_type=F32)
             for x, y, s, a, w in zip(q, e_b, st, att, vst)]
        kv = [jnp.dot(x.T.astype(BF16), (y * z).astype(BF16), preferred_element_type=F32)
              for x, y, z in zip(v, k, e_tail)]
        for bi in seqs:
            state_ref[bi] = st[bi] * e_b[bi][c - 1:c] + kv[bi] * smask

        sq = [_split_bf16(x * x) for x in o]
        ms = [(jnp.dot(hi, hones, preferred_element_type=F32)
               + jnp.dot(lo, hones, preferred_element_type=F32)) * (1.0 / GLA_DV) for hi, lo in sq]
        for bi in seqs:
            gg = x_ref[bi, rows, o3:o4]
            o_ref[bi, rows, :] = o[bi] * lax.rsqrt(ms[bi] + RMS_EPS) * norm * (gg * _sigmoid(gg))
        return 0

    lax.fori_loop(0, n_chunks, chunk, 0)

    @pl.when(tb == pl.num_programs(1) - 1)
    def _():
        sT_ref[...] = state_ref[...]


def _gla_state_to_kernel(s):
    b = s.shape[0]
    st = jnp.swapaxes(s, 2, 3)
    eye = jnp.eye(GLA_HEADS, dtype=s.dtype)
    full = st[:, :, :, None, :] * eye[None, :, None, :, None]
    return full.reshape(b, GLA_V_WIDTH, GLA_K_WIDTH)


def _gla_state_from_kernel(st):
    b = st.shape[0]
    full = st.reshape(b, GLA_HEADS, GLA_DV, GLA_HEADS, GLA_DK)
    diag = jnp.stack([full[:, h, :, h, :] for h in range(GLA_HEADS)], axis=1)
    return jnp.swapaxes(diag, 2, 3)


def _gla(gla3d, s0t, wg_bf, bg, norm, valid_len=None):
    batch, lp, _ = gla3d.shape
    tb = min(512, lp)
    bb = math.gcd(batch, GLA_BATCH)
    assert lp % tb == 0 and tb % GLA_CHUNK == 0
    sums, pmasks, smask, hones = _gla_constants()
    const2 = lambda b, t: (0, 0)
    est = 2 * bb * (tb * (GLA_COLS + GLA_V_WIDTH) * 4) + 4 * 1024 * 1024
    return pl.pallas_call(
        functools.partial(_gla_kernel, n_chunks=tb // GLA_CHUNK, valid_len=valid_len),
        grid=(batch // bb, lp // tb),
        in_specs=[pl.BlockSpec((bb, tb, GLA_COLS), lambda b, t: (b, t, 0)),
                  pl.BlockSpec((bb, GLA_V_WIDTH, GLA_K_WIDTH), lambda b, t: (b, 0, 0)),
                  pl.BlockSpec(sums.shape, const2),
                  pl.BlockSpec(pmasks.shape, lambda b, t: (0, 0, 0)),
                  pl.BlockSpec(smask.shape, const2),
                  pl.BlockSpec(hones.shape, const2),
                  pl.BlockSpec((GLR_PAD, GLA_K_WIDTH), const2),
                  pl.BlockSpec((1, GLA_K_WIDTH), const2),
                  pl.BlockSpec((1, GLA_V_WIDTH), const2)],
        out_specs=(pl.BlockSpec((bb, tb, GLA_V_WIDTH), lambda b, t: (b, t, 0)),
                   pl.BlockSpec((bb, GLA_V_WIDTH, GLA_K_WIDTH), lambda b, t: (b, 0, 0))),
        out_shape=(jax.ShapeDtypeStruct((batch, lp, GLA_V_WIDTH), F32),
                   jax.ShapeDtypeStruct((batch, GLA_V_WIDTH, GLA_K_WIDTH), F32)),
        scratch_shapes=[pltpu.VMEM((bb, GLA_V_WIDTH, GLA_K_WIDTH), F32)],
        compiler_params=pltpu.CompilerParams(
            dimension_semantics=("parallel", "arbitrary"), vmem_limit_bytes=_vmem_limit(est)),
        name="gla",
    )(gla3d, s0t, jnp.asarray(sums, BF16), jnp.asarray(pmasks), jnp.asarray(smask),
      jnp.asarray(hones, BF16), wg_bf, bg.reshape(1, GLA_K_WIDTH),
      jnp.tile(norm, GLA_HEADS).reshape(1, GLA_V_WIDTH))


def _gelu_tanh(x):
    return 0.5 * x * (1.0 + jnp.tanh(math.sqrt(2.0 / math.pi) * (x + 0.044715 * (x * x * x))))


def _lru_gates(xc, wa_ref, ba_ref, wx_ref, bx_ref, lam_ref):
    xb = xc.astype(BF16)
    r = _sigmoid(jnp.dot(xb, wa_ref[...], preferred_element_type=F32) + ba_ref[...])
    i = _sigmoid(jnp.dot(xb, wx_ref[...], preferred_element_type=F32) + bx_ref[...])
    log_a = LRU_C * r * _log_sigmoid(lam_ref[...])
    a = jnp.exp(log_a)
    one_minus_a2 = -jnp.tanh(log_a) * (a * a + 1.0)
    u = jnp.sqrt(one_minus_a2) * (i * xc)
    return a, u


def _shift_rows(x, d, fill):
    rows = lax.broadcasted_iota(jnp.int32, x.shape, 0)
    return jnp.where(rows >= d, pltpu.roll(x, d, axis=0), fill)


def _lru_prompt_kernel(x_ref, cw_ref, cb_ref, wa_ref, ba_ref, wx_ref, bx_ref, lam_ref, norm_ref,
                       o_ref, conv_ref, h_ref, *, seq):
    x = x_ref[0, :, 0:LRU_WIDTH]
    gate = x_ref[0, :, LRU_WIDTH:2 * LRU_WIDTH]
    xc = x * cw_ref[CONV_W - 1:CONV_W, :] + cb_ref[...]
    for j in range(1, CONV_W):
        xc = xc + _shift_rows(x, j, 0.0) * cw_ref[CONV_W - 1 - j:CONV_W - j, :]
    a, u = _lru_gates(xc, wa_ref, ba_ref, wx_ref, bx_ref, lam_ref)
    d = 1
    while d < seq:
        u = a * _shift_rows(u, d, 0.0) + u
        if 2 * d < seq:
            a = a * _shift_rows(a, d, 1.0)
        d *= 2
    o_ref[0] = _rms(u * _gelu_tanh(gate), norm_ref[...])
    conv_ref[0] = x[seq - (CONV_W - 1):seq, :]
    h_ref[0] = u[seq - 1:seq, :]


def _lru_prompt(lru3d, conv_w, conv_b, wa_bf, ba, wx_bf, bx, lam, norm):
    batch, seq, _ = lru3d.shape
    assert seq >= CONV_W - 1
    vec = lambda a: a.reshape(1, LRU_WIDTH)
    const = lambda b: (0, 0)
    vspec = pl.BlockSpec((1, LRU_WIDTH), const)
    wspec = pl.BlockSpec((LRU_WIDTH, LRU_WIDTH), const)
    est = 12 * seq * LRU_WIDTH * 4
    return pl.pallas_call(
        functools.partial(_lru_prompt_kernel, seq=seq),
        grid=(batch,),
        in_specs=[pl.BlockSpec((1, seq, LRU_COLS), lambda b: (b, 0, 0)),
                  pl.BlockSpec((CONV_W, LRU_WIDTH), const), vspec,
                  wspec, vspec, wspec, vspec, vspec, vspec],
        out_specs=(pl.BlockSpec((1, seq, LRU_WIDTH), lambda b: (b, 0, 0)),
                   pl.BlockSpec((1, CONV_W - 1, LRU_WIDTH), lambda b: (b, 0, 0)),
                   pl.BlockSpec((1, 1, LRU_WIDTH), lambda b: (b, 0, 0))),
        out_shape=(jax.ShapeDtypeStruct((batch, seq, LRU_WIDTH), F32),
                   jax.ShapeDtypeStruct((batch, CONV_W - 1, LRU_WIDTH), F32),
                   jax.ShapeDtypeStruct((batch, 1, LRU_WIDTH), F32)),
        compiler_params=pltpu.CompilerParams(
            dimension_semantics=("parallel",), vmem_limit_bytes=_vmem_limit(est)),
        name="lru_prompt",
    )(lru3d, conv_w, vec(conv_b), wa_bf, vec(ba), wx_bf, vec(bx), vec(lam), vec(norm))


def _lru_step_kernel(x_ref, buf_ref, h0_ref, cw_ref, cb_ref, wa_ref, ba_ref, wx_ref, bx_ref, lam_ref,
                     norm_ref, o_ref, conv_ref, h_ref):
    w = LRU_WIDTH
    x = x_ref[:, 0:w]
    gate = x_ref[:, w:2 * w]
    xc = x * cw_ref[CONV_W - 1:CONV_W, :] + cb_ref[...]
    for j in range(CONV_W - 1):
        xc = xc + buf_ref[:, j * w:(j + 1) * w] * cw_ref[j:j + 1, :]
    a, u = _lru_gates(xc, wa_ref, ba_ref, wx_ref, bx_ref, lam_ref)
    h = a * h0_ref[...] + u
    o_ref[...] = _rms(h * _gelu_tanh(gate), norm_ref[...])
    for j in range(CONV_W - 2):
        conv_ref[:, j * w:(j + 1) * w] = buf_ref[:, (j + 1) * w:(j + 2) * w]
    conv_ref[:, (CONV_W - 2) * w:(CONV_W - 1) * w] = x
    h_ref[...] = h


def _lru_step(lru2d, conv_buf, h0, conv_w, conv_b, wa_bf, ba, wx_bf, bx, lam, norm):
    batch = lru2d.shape[0]
    vec = lambda a: a.reshape(1, LRU_WIDTH)
    nbuf = (CONV_W - 1) * LRU_WIDTH
    o, conv, h = pl.pallas_call(
        _lru_step_kernel,
        out_shape=(jax.ShapeDtypeStruct((batch, LRU_WIDTH), F32),
                   jax.ShapeDtypeStruct((batch, nbuf), F32),
                   jax.ShapeDtypeStruct((batch, LRU_WIDTH), F32)),
        name="lru_step",
    )(lru2d, conv_buf.reshape(batch, nbuf), h0, conv_w, vec(conv_b), wa_bf, vec(ba), wx_bf, vec(bx),
      vec(lam), vec(norm))
    return o, conv.reshape(batch, CONV_W - 1, LRU_WIDTH), h


FF_CHUNK = 1024


def _mix_mlp_kernel(x_ref, sb_ref, gla_ref, lru_ref, sbn_ref, wout_ref, ln2_ref, wup_ref, wdn_ref,
                    lnf_ref, y_ref, *, final_norm):
    sb = _rms(sb_ref[...], sbn_ref[...]).astype(BF16)
    o1, o2 = SB_WIDTH, SB_WIDTH + GLA_V_WIDTH
    mix = (jnp.dot(sb, wout_ref[0:o1, :], preferred_element_type=F32)
           + jnp.dot(gla_ref[...].astype(BF16), wout_ref[o1:o2, :], preferred_element_type=F32)
           + jnp.dot(lru_ref[...].astype(BF16), wout_ref[o2:D_MODEL, :], preferred_element_type=F32))
    x1 = x_ref[...] + mix
    hm = _rms(x1, ln2_ref[...]).astype(BF16)
    acc = x1
    for c in range(D_FF // FF_CHUNK):
        up = jnp.dot(hm, wup_ref[:, c * FF_CHUNK:(c + 1) * FF_CHUNK], preferred_element_type=F32)
        act = jnp.square(jnp.maximum(up, 0.0)).astype(BF16)
        acc = acc + jnp.dot(act, wdn_ref[c * FF_CHUNK:(c + 1) * FF_CHUNK, :],
                            preferred_element_type=F32)
    y_ref[...] = _rms(acc, lnf_ref[...]) if final_norm else acc


def _mix_mlp(x2d, o_sb, o_gla, o_lru, sb_norm, wout_bf, ln2, wup_bf, wdn_bf, ln_f, final_norm):
    t = x2d.shape[0]
    tm = min(512, t)
    assert t % tm == 0
    row = lambda i: (i, 0)
    const = lambda i: (0, 0)
    single = pl.Buffered(1)
    est = ((D_MODEL * D_MODEL + 2 * D_MODEL * D_FF) * 2
           + 2 * tm * (2 * D_MODEL + SB_WIDTH + GLA_V_WIDTH + LRU_WIDTH) * 4
           + 4 * tm * FF_CHUNK * 4)
    return pl.pallas_call(
        functools.partial(_mix_mlp_kernel, final_norm=final_norm),
        grid=(t // tm,),
        in_specs=[pl.BlockSpec((tm, D_MODEL), row),
                  pl.BlockSpec((tm, SB_WIDTH), row),
                  pl.BlockSpec((tm, GLA_V_WIDTH), row),
                  pl.BlockSpec((tm, LRU_WIDTH), row),
                  pl.BlockSpec((1, SB_WIDTH), const),
                  pl.BlockSpec((D_MODEL, D_MODEL), const, pipeline_mode=single),
                  pl.BlockSpec((1, D_MODEL), const),
                  pl.BlockSpec((D_MODEL, D_FF), const, pipeline_mode=single),
                  pl.BlockSpec((D_FF, D_MODEL), const, pipeline_mode=single),
                  pl.BlockSpec((1, D_MODEL), const)],
        out_specs=pl.BlockSpec((tm, D_MODEL), row),
        out_shape=jax.ShapeDtypeStruct((t, D_MODEL), F32),
        compiler_params=pltpu.CompilerParams(
            dimension_semantics=("parallel",), vmem_limit_bytes=_vmem_limit(est)),
        name="mix_mlp",
    )(x2d, o_sb, o_gla, o_lru, sb_norm.reshape(1, SB_WIDTH), wout_bf, ln2.reshape(1, D_MODEL),
      wup_bf, wdn_bf, ln_f.reshape(1, D_MODEL))


def _block_diag(w):
    n, c, d = w.shape
    eye = jnp.eye(n, dtype=w.dtype)
    return (w[:, :, None, :] * eye[:, None, :, None]).reshape(n * c, n * d)


def _prep_layer(l, w):
    w_in = w["w_in"][l]
    split = SB_COLS + 2 * GLA_K_WIDTH + 2 * GLA_V_WIDTH + GLA_RANK
    w_in_p = jnp.concatenate(
        [w_in[:, :split], jnp.zeros((D_MODEL, GLR_PAD - GLA_RANK), w_in.dtype), w_in[:, split:]], axis=1)
    wg = jnp.concatenate(
        [w["gla_wg2"][l], jnp.zeros((GLR_PAD - GLA_RANK, GLA_K_WIDTH), F32)], axis=0)
    wkvt = jnp.transpose(w_in)[SB_WIDTH:3 * SB_WIDTH]
    return dict(
        ln1=w["ln1"][l], w_in=w_in_p.astype(BF16), wkvt=wkvt.astype(BF16),
        sb_bias=w["sb_bias"][l], sb_norm=w["sb_norm"][l],
        wg=wg.astype(BF16), bg=w["gla_bg"][l], gla_norm=w["gla_norm"][l],
        conv_w=w["conv_w"][l], conv_b=w["conv_b"][l],
        wa=_block_diag(w["lru_wa"][l]).astype(BF16), ba=w["lru_ba"][l],
        wx=_block_diag(w["lru_wx"][l]).astype(BF16), bx=w["lru_bx"][l],
        lam=w["lru_lambda"][l], lru_norm=w["lru_norm"][l],
        w_out=w["w_out"][l].astype(BF16), ln2=w["ln2"][l],
        w_up=w["w_up"][l].astype(BF16), w_down=w["w_down"][l].astype(BF16))


def _prompt_trunk(x, layers, ln_f):
    batch, seq, _ = x.shape
    depth = len(layers)
    t = batch * seq
    x2d = x.reshape(t, D_MODEL)
    ks, vs, ss, cs, hs = [], [], [], [], []
    s0t = jnp.zeros((batch, GLA_V_WIDTH, GLA_K_WIDTH), F32)
    lp = -(-seq // GLA_CHUNK) * GLA_CHUNK
    key_tile = min(SB_TILE, seq)

    def heads_last(xt):
        return jnp.transpose(xt.reshape(batch, SB_HEADS, SB_HEAD_DIM, seq), (0, 3, 1, 2))

    for l, p in enumerate(layers):
        qb, kt, vt, ktb, vtb, gla3d, lru3d = _in_proj_prompt(
            x2d.reshape(batch, seq, D_MODEL), p["ln1"], p["w_in"], p["wkvt"], key_tile)
        o_sb = _sb_attention_prompt(qb, ktb, vtb, p["sb_bias"]).reshape(t, SB_WIDTH)
        if lp != seq:
            gla3d = jnp.pad(gla3d, ((0, 0), (0, lp - seq), (0, 0)))
        o_gla, st = _gla(gla3d, s0t, p["wg"], p["bg"], p["gla_norm"],
                         valid_len=None if lp == seq else seq)
        o_gla = o_gla[:, :seq].reshape(t, GLA_V_WIDTH)
        o_lru, conv, h_last = _lru_prompt(lru3d, p["conv_w"], p["conv_b"],
                                          p["wa"], p["ba"], p["wx"], p["bx"], p["lam"], p["lru_norm"])
        x2d = _mix_mlp(x2d, o_sb, o_gla, o_lru.reshape(t, LRU_WIDTH), p["sb_norm"], p["w_out"],
                       p["ln2"], p["w_up"], p["w_down"], ln_f, final_norm=(l == depth - 1))
        ks.append(heads_last(kt))
        vs.append(heads_last(vt))
        ss.append(_gla_state_from_kernel(st))
        cs.append(conv)
        hs.append(h_last.reshape(batch, LRU_WIDTH))
    return (x2d.reshape(batch, seq, D_MODEL), jnp.stack(ks), jnp.stack(vs), jnp.stack(ss),
            jnp.stack(cs), jnp.stack(hs))


def _sample_trunk(x, cache_k, cache_v, page_table, gla_s0, conv0, lru0, layers, ln_f):
    batch, seq, _ = x.shape
    assert seq == 1
    depth = len(layers)
    x2d = x.reshape(batch, D_MODEL)
    ks, vs, ss, cs, hs = [], [], [], [], []
    for l, p in enumerate(layers):
        qb, k, v, gla, lru = _in_proj_step(x2d, p["ln1"], p["w_in"])
        o_sb = _sb_attention_decode(qb, cache_k, cache_v, page_table, p["sb_bias"], l)
        gla3d = jnp.pad(gla.reshape(batch, 1, GLA_COLS), ((0, 0), (0, GLA_CHUNK - 1), (0, 0)))
        o_gla, st = _gla(gla3d, _gla_state_to_kernel(gla_s0[l]), p["wg"], p["bg"], p["gla_norm"],
                         valid_len=1)
        o_lru, conv, h_last = _lru_step(lru, conv0[l], lru0[l], p["conv_w"], p["conv_b"], p["wa"],
                                        p["ba"], p["wx"], p["bx"], p["lam"], p["lru_norm"])
        x2d = _mix_mlp(x2d, o_sb, o_gla[:, 0], o_lru, p["sb_norm"], p["w_out"], p["ln2"], p["w_up"],
                       p["w_down"], ln_f, final_norm=(l == depth - 1))
        ks.append(k.reshape(batch, 1, SB_HEADS, SB_HEAD_DIM))
        vs.append(v.reshape(batch, 1, SB_HEADS, SB_HEAD_DIM))
        ss.append(_gla_state_from_kernel(st))
        cs.append(conv)
        hs.append(h_last)
    return (x2d.reshape(batch, 1, D_MODEL), jnp.stack(ks), jnp.stack(vs), jnp.stack(ss),
            jnp.stack(cs), jnp.stack(hs))


def kernel(x_prompt, x_sample, cache_sb_k, cache_sb_v, page_table, state_gla, state_conv, state_lru, ln1, w_in, sb_bias, sb_norm, gla_wg2, gla_bg, gla_norm, conv_w, conv_b, lru_wa, lru_ba, lru_wx, lru_bx, lru_lambda, lru_norm, w_out, ln2, w_up, w_down, ln_f):
    w = dict(ln1=ln1, w_in=w_in, sb_bias=sb_bias, sb_norm=sb_norm, gla_wg2=gla_wg2, gla_bg=gla_bg,
             gla_norm=gla_norm, conv_w=conv_w, conv_b=conv_b, lru_wa=lru_wa, lru_ba=lru_ba,
             lru_wx=lru_wx, lru_bx=lru_bx, lru_lambda=lru_lambda, lru_norm=lru_norm, w_out=w_out,
             ln2=ln2, w_up=w_up, w_down=w_down)
    layers = [_prep_layer(l, w) for l in range(ln1.shape[0])]
    y_p, k_p, v_p, gla_p, conv_p, lru_p = _prompt_trunk(x_prompt, layers, ln_f)
    y_s, k_s, v_s, gla_s, conv_s, lru_s = _sample_trunk(
        x_sample, cache_sb_k, cache_sb_v, page_table, state_gla, state_conv, state_lru, layers, ln_f)
    return (y_p, y_s, k_p, v_p, k_s, v_s, gla_p, gla_s, conv_p, conv_s, lru_p, lru_s)
```
